```python
import jax, jax.numpy as jnp
from jax import lax
import numpy as np

D_MODEL = 2048
BATCH = 8
SEQ = 2048
DEPTH = 2

GRID_W = 64
CTX_LEN = 256
D_FOURIER = D_MODEL // 2
FOURIER_GROUPS = 4
D_LRU = D_MODEL // 2
LRU_HEADS = 8
LRU_HEAD_DIM = D_LRU // LRU_HEADS
CONV_WIDTH = 4
CONV_LEFT = 2
LRU_C = 8.0
N_IN = D_FOURIER + 2 * D_LRU + 2 * D_MODEL
D_FF = 5632
N_EXPERTS = 8
TOP_K = 2
D_EXPERT = 7168
N_DENSE = (DEPTH + 1) // 2
N_MOE = DEPTH // 2
NORM_EPS = 1e-6

kernel_name = "hybrid_fnet_rglru_moe_diffusion_trunk"


def rmsnorm(h, g):
    hf = h.astype(jnp.float32)
    y = hf * lax.rsqrt(jnp.mean(hf * hf, axis=-1, keepdims=True) + NORM_EPS)
    return (y * g.astype(jnp.float32)).astype(h.dtype)


def modulate(h, shift, scale):
    return h * (1.0 + scale) + shift


def ada_mod(cvec, w, b):
    m = jax.nn.silu(cvec) @ w + b
    return jnp.split(m, 6, axis=-1)


def fourier_mix(u):
    B, L, _ = u.shape
    ug = u.astype(jnp.float32).reshape(B, L, FOURIER_GROUPS, D_FOURIER // FOURIER_GROUPS)
    y = jnp.fft.fftn(ug, axes=(1, 3), norm="ortho").real
    return y.reshape(B, L, D_FOURIER).astype(u.dtype)


def dwconv(u, w, b):
    L = u.shape[-2]
    pad = [(0, 0)] * (u.ndim - 2) + [(CONV_LEFT, CONV_WIDTH - 1 - CONV_LEFT), (0, 0)]
    up = jnp.pad(u, pad)
    y = up[..., 0:L, :] * w[0]
    for k in range(1, CONV_WIDTH):
        y = y + up[..., k:k + L, :] * w[k]
    return y + b


def lru_conv_input(ur, conv_w, conv_b, grid):
    if grid:
        B, L, C = ur.shape
        rows = L // GRID_W
        return dwconv(ur.reshape(B, rows, GRID_W, C), conv_w, conv_b).reshape(B, L, C)
    return dwconv(ur, conv_w, conv_b)


def _lin_combine(e1, e2):
    a1, b1 = e1
    a2, b2 = e2
    return a1 * a2, a2 * b1 + b2


def rglru(v, wa, ba, wx, bx, lam, h0, reverse):
    B, L, _ = v.shape
    vf = v.astype(jnp.float32)
    vh = vf.reshape(B, L, LRU_HEADS, LRU_HEAD_DIM)
    r = jax.nn.sigmoid(jnp.einsum('blhi,hij->blhj', vh, wa.astype(jnp.float32)).reshape(B, L, D_LRU) + ba.astype(jnp.float32))
    i = jax.nn.sigmoid(jnp.einsum('blhi,hij->blhj', vh, wx.astype(jnp.float32)).reshape(B, L, D_LRU) + bx.astype(jnp.float32))
    log_a = -LRU_C * r * jax.nn.softplus(-lam.astype(jnp.float32))
    a = jnp.exp(log_a)
    bterm = jnp.sqrt(-jnp.expm1(2.0 * log_a)) * (i * vf)
    A, Bc = lax.associative_scan(_lin_combine, (a, bterm), axis=1, reverse=reverse)
    h = Bc + A * h0[:, None, :]
    h_last = h[:, 0] if reverse else h[:, -1]
    return h, h_last


def mixer(h, w_in, conv_w, conv_b, wa, ba, wx, bx, lam, w_fo, w_ro, w_o, h0f, h0b, grid):
    proj = h @ w_in
    o1 = D_FOURIER
    o2 = o1 + D_LRU
    o3 = o2 + D_LRU
    o4 = o3 + D_MODEL
    uf, ur, ug, gf, gr = proj[..., :o1], proj[..., o1:o2], proj[..., o2:o3], proj[..., o3:o4], proj[..., o4:]
    y_f = fourier_mix(uf) @ w_fo
    v = lru_conv_input(ur, conv_w, conv_b, grid)
    hf, hTf = rglru(v, wa[0], ba[0], wx[0], bx[0], lam[0], h0f, False)
    hb, hTb = rglru(v, wa[1], ba[1], wx[1], bx[1], lam[1], h0b, True)
    y_r = ((hf + hb).astype(h.dtype) * jax.nn.gelu(ug)) @ w_ro
    merged = jax.nn.sigmoid(gf) * y_f + jax.nn.sigmoid(gr) * y_r
    return merged @ w_o, hTf, hTb


def context_lru_states(h, w_in, conv_w, conv_b, wa, ba, wx, bx, lam, h0):
    ur = h @ w_in[:, D_FOURIER:D_FOURIER + D_LRU]
    v = lru_conv_input(ur, conv_w, conv_b, False)
    _, hTf = rglru(v, wa[0], ba[0], wx[0], bx[0], lam[0], h0, False)
    _, hTb = rglru(v, wa[1], ba[1], wx[1], bx[1], lam[1], h0, True)
    return hTf, hTb


def swiglu(h, wg, wu, wd):
    return (jax.nn.silu(h @ wg) * (h @ wu)) @ wd


def moe_swiglu(h, router, wg, wu, wd):
    B, L, D = h.shape
    t = h.reshape(B * L, D)
    logits = (t @ router).astype(jnp.float32)
    top_v, top_i = lax.top_k(logits, TOP_K)
    gates = jax.nn.softmax(top_v, axis=-1)
    combine = jnp.sum(jax.nn.one_hot(top_i, N_EXPERTS, dtype=jnp.float32) * gates[..., None], axis=1)
    out = jnp.zeros_like(t)
    for e in range(N_EXPERTS):
        out = out + combine[:, e:e + 1].astype(t.dtype) * swiglu(t, wg[e], wu[e], wd[e])
    return out.reshape(B, L, D)


def channel_mixer(h, l, ffn_w_gate, ffn_w_up, ffn_w_down, moe_router, moe_w_gate, moe_w_up, moe_w_down):
    if l % 2 == 0:
        j = l // 2
        return swiglu(h, ffn_w_gate[j], ffn_w_up[j], ffn_w_down[j])
    j = l // 2
    return moe_swiglu(h, moe_router[j], moe_w_gate[j], moe_w_up[j], moe_w_down[j])


def setup_inputs(seed: int = 0) -> dict:
    key = jax.random.key(seed)
    ks = jax.random.split(key, 32)
    f32 = jnp.float32
    nrm = lambda k, shape, s: jax.random.normal(k, shape, f32) * s
    u = jax.random.uniform(ks[15], (DEPTH, 2, D_LRU), f32, 0.9, 0.999)
    s = u ** (1.0 / LRU_C)
    lam = jnp.log(s) - jnp.log1p(-s)
    return {
        "x": nrm(ks[0], (BATCH, SEQ, D_MODEL), 1.0),
        "c": nrm(ks[1], (BATCH, D_MODEL), 1.0),
        "ctx": nrm(ks[2], (BATCH, CTX_LEN, D_MODEL), 1.0),
        "c_ctx": nrm(ks[3], (D_MODEL,), 1.0),
        "ada_w": nrm(ks[4], (DEPTH, D_MODEL, 6 * D_MODEL), 0.5 * D_MODEL ** -0.5),
        "ada_b": nrm(ks[5], (DEPTH, 6 * D_MODEL), 0.02),
        "norm1_g": 1.0 + nrm(ks[6], (DEPTH, D_MODEL), 0.02),
        "norm2_g": 1.0 + nrm(ks[7], (DEPTH, D_MODEL), 0.02),
        "w_in": nrm(ks[8], (DEPTH, D_MODEL, N_IN), D_MODEL ** -0.5),
        "conv_w": nrm(ks[9], (DEPTH, CONV_WIDTH, D_LRU), CONV_WIDTH ** -0.5),
        "conv_b": nrm(ks[10], (DEPTH, D_LRU), 0.02),
        "lru_wa": nrm(ks[11], (DEPTH, 2, LRU_HEADS, LRU_HEAD_DIM, LRU_HEAD_DIM), LRU_HEAD_DIM ** -0.5),
        "lru_ba": nrm(ks[12], (DEPTH, 2, D_LRU), 0.02),
        "lru_wx": nrm(ks[13], (DEPTH, 2, LRU_HEADS, LRU_HEAD_DIM, LRU_HEAD_DIM), LRU_HEAD_DIM ** -0.5),
        "lru_bx": nrm(ks[14], (DEPTH, 2, D_LRU), 0.02),
        "lru_lambda": lam,
        "w_fourier_out": nrm(ks[16], (DEPTH, D_FOURIER, D_MODEL), D_FOURIER ** -0.5),
        "w_lru_out": nrm(ks[17], (DEPTH, D_LRU, D_MODEL), D_LRU ** -0.5),
        "w_out": nrm(ks[18], (DEPTH, D_MODEL, D_MODEL), D_MODEL ** -0.5),
        "ffn_w_gate": nrm(ks[19], (N_DENSE, D_MODEL, D_FF), D_MODEL ** -0.5),
        "ffn_w_up": nrm(ks[20], (N_DENSE, D_MODEL, D_FF), D_MODEL ** -0.5),
        "ffn_w_down": nrm(ks[21], (N_DENSE, D_FF, D_MODEL), D_FF ** -0.5),
        "moe_router": nrm(ks[22], (N_MOE, D_MODEL, N_EXPERTS), D_MODEL ** -0.5),
        "moe_w_gate": nrm(ks[23], (N_MOE, N_EXPERTS, D_MODEL, D_EXPERT), D_MODEL ** -0.5),
        "moe_w_up": nrm(ks[24], (N_MOE, N_EXPERTS, D_MODEL, D_EXPERT), D_MODEL ** -0.5),
        "moe_w_down": nrm(ks[25], (N_MOE, N_EXPERTS, D_EXPERT, D_MODEL), D_EXPERT ** -0.5),
        "final_norm_g": 1.0 + nrm(ks[26], (D_MODEL,), 0.02),
    }


def reference(x, c, ctx, c_ctx, ada_w, ada_b, norm1_g, norm2_g, w_in, conv_w, conv_b,
              lru_wa, lru_ba, lru_wx, lru_bx, lru_lambda, w_fourier_out, w_lru_out, w_out,
              ffn_w_gate, ffn_w_up, ffn_w_down, moe_router, moe_w_gate, moe_w_up, moe_w_down,
              final_norm_g):
    lat = x
    cx = ctx
    B = x.shape[0]
    h_zero = jnp.zeros((B, D_LRU), jnp.float32)
    for l in range(DEPTH):
        last = l == DEPTH - 1
        sh1, sc1, g1, sh2, sc2, g2 = [m[:, None, :] for m in ada_mod(c, ada_w[l], ada_b[l])]
        csh1, csc1, cg1, csh2, csc2, cg2 = ada_mod(c_ctx, ada_w[l], ada_b[l])
        lp = (w_in[l], conv_w[l], conv_b[l], lru_wa[l], lru_ba[l], lru_wx[l], lru_bx[l], lru_lambda[l])
        hc = modulate(rmsnorm(cx, norm1_g[l]), csh1, csc1)
        if last:
            hTf, hTb = context_lru_states(hc, *lp, h_zero)
        else:
            oc, hTf, hTb = mixer(hc, *lp, w_fourier_out[l], w_lru_out[l], w_out[l], h_zero, h_zero, False)
            cx = cx + cg1 * oc
        hl = modulate(rmsnorm(lat, norm1_g[l]), sh1, sc1)
        ol, _, _ = mixer(hl, *lp, w_fourier_out[l], w_lru_out[l], w_out[l], hTf, hTb, True)
        lat = lat + g1 * ol
        fp = (ffn_w_gate, ffn_w_up, ffn_w_down, moe_router, moe_w_gate, moe_w_up, moe_w_down)
        if not last:
            cx = cx + cg2 * channel_mixer(modulate(rmsnorm(cx, norm2_g[l]), csh2, csc2), l, *fp)
        lat = lat + g2 * channel_mixer(modulate(rmsnorm(lat, norm2_g[l]), sh2, sc2), l, *fp)
    return rmsnorm(lat, final_norm_g)
```

```python
import functools

import numpy as np
import jax
import jax.numpy as jnp
from jax import lax
from jax.experimental import pallas as pl
from jax.experimental.pallas import tpu as pltpu

F32 = jnp.float32
BF16 = jnp.bfloat16

GRID_W = 64
FOURIER_GROUPS = 4
LRU_HEAD_DIM = 128
CONV_LEFT = 2
CONV_WIDTH = 4
LRU_C = 8.0
SCAN_UNROLL = 8
DMA_ISSUE_UNROLL = 8
TOP_K = 2
NORM_EPS = 1e-6

VMEM_LIMIT_BYTES = 56 * 1024 * 1024
LANES = 128
SUBLANES = 8


def _params(*sem):
    return pltpu.CompilerParams(dimension_semantics=sem, vmem_limit_bytes=VMEM_LIMIT_BYTES)


def _tile(dim, pref, align):
    if dim <= pref:
        return dim
    t = (pref // align) * align
    while t >= align:
        if dim % t == 0:
            return t
        t -= align
    return dim


def _ada_kernel(c_ref, w_ref, b_ref, o_ref):
    c = c_ref[...]
    a = (c * jax.nn.sigmoid(c)).astype(BF16)
    o_ref[...] = jnp.dot(a, w_ref[...].astype(BF16), preferred_element_type=F32) + b_ref[...]


def _ada_mod(cc, ada_w, ada_b):
    depth, d, n = ada_w.shape
    rows = cc.shape[0]
    tn = _tile(n, 1024, LANES)
    return pl.pallas_call(
        _ada_kernel,
        grid=(depth, n // tn),
        in_specs=[
            pl.BlockSpec((rows, d), lambda l, j: (0, 0)),
            pl.BlockSpec((None, d, tn), lambda l, j: (l, 0, j)),
            pl.BlockSpec((None, 1, tn), lambda l, j: (l, 0, j)),
        ],
        out_specs=pl.BlockSpec((None, rows, tn), lambda l, j: (l, 0, j)),
        out_shape=jax.ShapeDtypeStruct((depth, rows, n), F32),
        compiler_params=_params("arbitrary", "arbitrary"),
        name="ada_mod",
    )(cc, ada_w, ada_b.reshape(depth, 1, n))


def _rms(x, g):
    return x * lax.rsqrt(jnp.mean(x * x, axis=-1, keepdims=True) + NORM_EPS) * g


def _norm_mod_kernel(x_ref, g_ref, sh_ref, sc_ref, o_ref):
    y = _rms(x_ref[...], g_ref[...])
    o_ref[...] = (y * (1.0 + sc_ref[...]) + sh_ref[...]).astype(o_ref.dtype)


def _norm_mod(x, g, shift, scale, seq):
    m, d = x.shape
    tm = _tile(seq, 256, SUBLANES)
    per = seq // tm
    nb = shift.shape[0]
    bidx = (lambda i: (i // per, 0, 0)) if nb > 1 else (lambda i: (0, 0, 0))
    return pl.pallas_call(
        _norm_mod_kernel,
        grid=(m // tm,),
        in_specs=[
            pl.BlockSpec((tm, d), lambda i: (i, 0)),
            pl.BlockSpec((1, d), lambda i: (0, 0)),
            pl.BlockSpec((None, 1, d), bidx),
            pl.BlockSpec((None, 1, d), bidx),
        ],
        out_specs=pl.BlockSpec((tm, d), lambda i: (i, 0)),
        out_shape=jax.ShapeDtypeStruct((m, d), BF16),
        compiler_params=_params("arbitrary"),
        name="norm_mod",
    )(x, g.reshape(1, d), shift, scale)


def _mm_kernel(a_ref, w_ref, o_ref):
    o_ref[...] = jnp.dot(a_ref[...].astype(BF16), w_ref[...],
                         preferred_element_type=F32).astype(o_ref.dtype)


def _matmul(a, w, out_dtype, *, tm_pref=1024, tn_pref=1024, a_col_block=0):
    m = a.shape[0]
    k, n = w.shape
    tm = _tile(m, tm_pref, SUBLANES)
    tn = _tile(n, tn_pref, LANES)
    return pl.pallas_call(
        _mm_kernel,
        grid=(m // tm, n // tn),
        in_specs=[
            pl.BlockSpec((tm, k), lambda i, j: (i, a_col_block)),
            pl.BlockSpec((k, tn), lambda i, j: (0, j)),
        ],
        out_specs=pl.BlockSpec((tm, tn), lambda i, j: (i, j)),
        out_shape=jax.ShapeDtypeStruct((m, n), out_dtype),
        compiler_params=_params("arbitrary", "arbitrary"),
        name="matmul",
    )(a, w)


def _dft_cos_sin(n):
    j = jnp.arange(n, dtype=jnp.int32)
    ang = ((j[:, None] * j[None, :]) % n).astype(F32) * (2.0 * np.pi / n)
    return jnp.cos(ang), jnp.sin(ang)


def _fourier_ch_kernel(a_ref, w_ref, uc_ref, us_ref):
    gs = uc_ref.shape[-1]
    z = jnp.dot(a_ref[...].astype(BF16), w_ref[...], preferred_element_type=F32)
    uc_ref[...] = z[:, :gs].astype(uc_ref.dtype)
    us_ref[...] = z[:, gs:].astype(us_ref.dtype)


def _fourier_ch(proj, d_f):
    m = proj.shape[0]
    gs = d_f // FOURIER_GROUPS
    c, s = _dft_cos_sin(gs)
    w = jnp.concatenate([c, s], axis=1).astype(BF16)
    tm = _tile(m, 1024, SUBLANES)
    out = jax.ShapeDtypeStruct((m, d_f), BF16)
    return pl.pallas_call(
        _fourier_ch_kernel,
        grid=(m // tm, FOURIER_GROUPS),
        in_specs=[
            pl.BlockSpec((tm, gs), lambda i, g: (i, g)),
            pl.BlockSpec((gs, 2 * gs), lambda i, g: (0, 0)),
        ],
        out_specs=[pl.BlockSpec((tm, gs), lambda i, g: (i, g)),
                   pl.BlockSpec((tm, gs), lambda i, g: (i, g))],
        out_shape=[out, out],
        compiler_params=_params("arbitrary", "arbitrary"),
        name="fourier_ch",
    )(proj, w)


def _fourier_seq_kernel(c_ref, s_ref, uc_ref, us_ref, o_ref, *, scale):
    acc = jnp.dot(c_ref[...], uc_ref[...], preferred_element_type=F32)
    acc = acc - jnp.dot(s_ref[...], us_ref[...], preferred_element_type=F32)
    o_ref[...] = (acc * scale).astype(o_ref.dtype)


def _fourier_seq(uc, us, seq):
    m, d_f = uc.shape
    nb = m // seq
    gs = d_f // FOURIER_GROUPS
    c, s = _dft_cos_sin(seq)
    tm = _tile(seq, 1024, SUBLANES)
    tn = _tile(d_f, 512, LANES)
    per = seq // tm
    scale = float(1.0 / np.sqrt(float(seq) * float(gs)))
    return pl.pallas_call(
        functools.partial(_fourier_seq_kernel, scale=scale),
        grid=(nb, per, d_f // tn),
        in_specs=[
            pl.BlockSpec((tm, seq), lambda b, i, j: (i, 0)),
            pl.BlockSpec((tm, seq), lambda b, i, j: (i, 0)),
            pl.BlockSpec((seq, tn), lambda b, i, j: (b, j)),
            pl.BlockSpec((seq, tn), lambda b, i, j: (b, j)),
        ],
        out_specs=pl.BlockSpec((tm, tn), lambda b, i, j: (b * per + i, j)),
        out_shape=jax.ShapeDtypeStruct((m, d_f), BF16),
        compiler_params=_params("arbitrary", "arbitrary", "arbitrary"),
        name="fourier_seq",
    )(c.astype(BF16), s.astype(BF16), uc, us)


def _lru_kernel(ur_ref, ug_ref, cw_ref, cb_ref, wg_ref, bg_ref, lam_ref, h0f_ref, h0b_ref,
                z_ref, htf_ref, htb_ref,
                af_s, bf_s, ab_s, bb_s, rf_s, rb_s, *, seq, conv_w, want_z):
    hd = LRU_HEAD_DIM
    seg = seq // SUBLANES
    pitch = seg + SUBLANES

    u = ur_ref[...]
    t = lax.broadcasted_iota(jnp.int32, (seq, hd), 0)
    col = t % conv_w

    def tap(k):
        rolled = pltpu.roll(u, shift=(-k) % seq, axis=0)
        ok = jnp.logical_and(col + k >= 0, col + k < conv_w)
        return jnp.where(ok, rolled, 0.0)

    cw = cw_ref[...]
    v = tap(-CONV_LEFT) * cw[0:1]
    for k in range(1, CONV_WIDTH):
        src = u if k == CONV_LEFT else tap(k - CONV_LEFT)
        v = v + src * cw[k:k + 1]
    v = v + cb_ref[...]

    gates = jnp.dot(v.astype(BF16), wg_ref[...], preferred_element_type=F32)
    bg = bg_ref[...]
    lam = lam_ref[...]
    neg = -lam
    softplus = jnp.maximum(neg, 0.0) + jnp.log1p(jnp.exp(-jnp.abs(neg)))

    def coeffs(d):
        r = jax.nn.sigmoid(gates[:, (2 * d) * hd:(2 * d + 1) * hd] + bg[2 * d:2 * d + 1])
        i = jax.nn.sigmoid(gates[:, (2 * d + 1) * hd:(2 * d + 2) * hd] + bg[2 * d + 1:2 * d + 2])
        log_a = (-LRU_C) * r * softplus[d:d + 1]
        a = jnp.exp(log_a)
        b = jnp.sqrt(1.0 - a * a) * (i * v)
        return a, b

    a_f, b_f = coeffs(0)
    a_b, b_b = coeffs(1)
    for s in range(SUBLANES):
        src = slice(s * seg, (s + 1) * seg)
        dst = slice(s * pitch, s * pitch + seg)
        af_s[dst, :] = a_f[src]
        bf_s[dst, :] = b_f[src]
        ab_s[dst, :] = a_b[src]
        bb_s[dst, :] = b_b[src]

    def rows(ref, j):
        return ref[pl.ds(j, SUBLANES, stride=pitch), :]

    zero = jnp.zeros((SUBLANES, hd), F32)
    one = jnp.ones((SUBLANES, hd), F32)

    def totals(j, carry):
        hf, pf, hb, pb = carry
        jb = seg - 1 - j
        a = rows(af_s, j)
        hf = a * hf + rows(bf_s, j)
        pf = a * pf
        a = rows(ab_s, jb)
        hb = a * hb + rows(bb_s, jb)
        pb = a * pb
        return hf, pf, hb, pb

    hf, pf, hb, pb = lax.fori_loop(0, seg, totals, (zero, one, zero, one), unroll=SCAN_UNROLL)

    cf = [h0f_ref[...]]
    for s in range(1, SUBLANES):
        cf.append(pf[s - 1:s] * cf[-1] + hf[s - 1:s])
    cb = [h0b_ref[...]]
    for s in range(SUBLANES - 2, -1, -1):
        cb.append(pb[s + 1:s + 2] * cb[-1] + hb[s + 1:s + 2])
    cb = cb[::-1]
    htf_ref[...] = pf[SUBLANES - 1:SUBLANES] * cf[-1] + hf[SUBLANES - 1:SUBLANES]
    htb_ref[...] = pb[0:1] * cb[0] + hb[0:1]

    if want_z:
        cf = jnp.concatenate(cf, axis=0)
        cb = jnp.concatenate(cb, axis=0)

        def states(j, carry):
            hf, hb = carry
            jb = seg - 1 - j
            hf = rows(af_s, j) * hf + rows(bf_s, j)
            rf_s[pl.ds(j, SUBLANES, stride=pitch), :] = hf
            hb = rows(ab_s, jb) * hb + rows(bb_s, jb)
            rb_s[pl.ds(jb, SUBLANES, stride=pitch), :] = hb
            return hf, hb

        lax.fori_loop(0, seg, states, (cf, cb), unroll=SCAN_UNROLL)

        gate = jax.nn.gelu(ug_ref[...])
        for s in range(SUBLANES):
            src = slice(s * seg, (s + 1) * seg)
            dst = slice(s * pitch, s * pitch + seg)
            z_ref[src, :] = ((rf_s[dst, :] + rb_s[dst, :]) * gate[src]).astype(z_ref.dtype)
    else:
        z_ref[...] = jnp.zeros(z_ref.shape, z_ref.dtype)


def _lru(proj, ur_col0, ug_col0, conv_w_l, conv_b_l, wa, ba, wx, bx, lam, h0f, h0b, *,
         seq, conv_row, want_z):
    m = proj.shape[0]
    nb = m // seq
    hd = LRU_HEAD_DIM
    heads = wa.shape[1]
    d_lru = heads * hd
    seg = seq // SUBLANES
    pitch = seg + SUBLANES
    wg = jnp.concatenate([wa[0], wx[0], wa[1], wx[1]], axis=-1).astype(BF16)
    bg = jnp.stack([ba[0], bx[0], ba[1], bx[1]], axis=0)
    state = jax.ShapeDtypeStruct((nb, 1, d_lru), F32)
    cblk = lambda c0: (lambda b, h: (b, c0 // hd + h))
    z, htf, htb = pl.pallas_call(
        functools.partial(_lru_kernel, seq=seq, conv_w=conv_row, want_z=want_z),
        grid=(nb, heads),
        in_specs=[
            pl.BlockSpec((seq, hd), cblk(ur_col0)),
            pl.BlockSpec((seq, hd), cblk(ug_col0)),
            pl.BlockSpec((CONV_WIDTH, hd), lambda b, h: (0, h)),
            pl.BlockSpec((1, hd), lambda b, h: (0, h)),
            pl.BlockSpec((None, hd, 4 * hd), lambda b, h: (h, 0, 0)),
            pl.BlockSpec((4, hd), lambda b, h: (0, h)),
            pl.BlockSpec((2, hd), lambda b, h: (0, h)),
            pl.BlockSpec((None, 1, hd), lambda b, h: (b, 0, h)),
            pl.BlockSpec((None, 1, hd), lambda b, h: (b, 0, h)),
        ],
        out_specs=[
            pl.BlockSpec((seq, hd), lambda b, h: (b, h)),
            pl.BlockSpec((None, 1, hd), lambda b, h: (b, 0, h)),
            pl.BlockSpec((None, 1, hd), lambda b, h: (b, 0, h)),
        ],
        out_shape=[jax.ShapeDtypeStruct((m, d_lru), BF16), state, state],
        scratch_shapes=[pltpu.VMEM((SUBLANES * pitch, hd), F32) for _ in range(6)],
        compiler_params=_params("arbitrary", "arbitrary"),
        name="lru",
    )(proj, proj, conv_w_l, conv_b_l.reshape(1, d_lru), wg, bg, lam,
      h0f.reshape(nb, 1, d_lru), h0b.reshape(nb, 1, d_lru))
    return z, htf.reshape(nb, d_lru), htb.reshape(nb, d_lru)


def _merge_kernel(y_ref, z_ref, wf_ref, wr_ref, gf_ref, gr_ref, o_ref):
    yf = jnp.dot(y_ref[...], wf_ref[...], preferred_element_type=F32)
    yr = jnp.dot(z_ref[...], wr_ref[...], preferred_element_type=F32)
    o_ref[...] = (jax.nn.sigmoid(gf_ref[...]) * yf
                  + jax.nn.sigmoid(gr_ref[...]) * yr).astype(o_ref.dtype)


def _merge(y, z, w_fo, w_ro, proj, gf_col0, gr_col0):
    m, d_f = y.shape
    d_lru = z.shape[1]
    d = w_fo.shape[1]
    tm = _tile(m, 1024, SUBLANES)
    tn = _tile(int(np.gcd(np.gcd(gf_col0, gr_col0), d)), 512, LANES)
    return pl.pallas_call(
        _merge_kernel,
        grid=(m // tm, d // tn),
        in_specs=[
            pl.BlockSpec((tm, d_f), lambda i, j: (i, 0)),
            pl.BlockSpec((tm, d_lru), lambda i, j: (i, 0)),
            pl.BlockSpec((d_f, tn), lambda i, j: (0, j)),
            pl.BlockSpec((d_lru, tn), lambda i, j: (0, j)),
            pl.BlockSpec((tm, tn), lambda i, j: (i, gf_col0 // tn + j)),
            pl.BlockSpec((tm, tn), lambda i, j: (i, gr_col0 // tn + j)),
        ],
        out_specs=pl.BlockSpec((tm, tn), lambda i, j: (i, j)),
        out_shape=jax.ShapeDtypeStruct((m, d), BF16),
        compiler_params=_params("arbitrary", "arbitrary"),
        name="merge",
    )(y, z, w_fo, w_ro, proj, proj)


def _mm_res_kernel(a_ref, w_ref, res_ref, gate_ref, o_ref):
    acc = jnp.dot(a_ref[...], w_ref[...], preferred_element_type=F32)
    o_ref[...] = res_ref[...] + gate_ref[...] * acc


def _matmul_res(a, w, res, gate, seq):
    m, k = a.shape
    n = w.shape[1]
    tm = _tile(seq, 1024, SUBLANES)
    tn = _tile(n, 1024, LANES)
    per = seq // tm
    nb = gate.shape[0]
    gidx = (lambda i, j: (i // per, 0, j)) if nb > 1 else (lambda i, j: (0, 0, j))
    return pl.pallas_call(
        _mm_res_kernel,
        grid=(m // tm, n // tn),
        in_specs=[
            pl.BlockSpec((tm, k), lambda i, j: (i, 0)),
            pl.BlockSpec((k, tn), lambda i, j: (0, j)),
            pl.BlockSpec((tm, tn), lambda i, j: (i, j)),
            pl.BlockSpec((None, 1, tn), gidx),
        ],
        out_specs=pl.BlockSpec((tm, tn), lambda i, j: (i, j)),
        out_shape=jax.ShapeDtypeStruct((m, n), F32),
        compiler_params=_params("arbitrary", "arbitrary"),
        name="matmul_res",
    )(a, w, res, gate)


def _ffn_kernel(eid_ref, nact_ref, x_ref, wg_ref, wu_ref, wd_ref, *rest, gated_residual):
    if gated_residual:
        res_ref, gate_ref, o_ref, acc_ref = rest
    else:
        o_ref, acc_ref = rest
    i = pl.program_id(0)
    f = pl.program_id(1)

    @pl.when(i < nact_ref[0])
    def _():
        @pl.when(f == 0)
        def _():
            acc_ref[...] = jnp.zeros(acc_ref.shape, acc_ref.dtype)

        x = x_ref[...].astype(BF16)
        g = jnp.dot(x, wg_ref[...], preferred_element_type=F32)
        u = jnp.dot(x, wu_ref[...], preferred_element_type=F32)
        act = ((g * jax.nn.sigmoid(g)) * u).astype(BF16)
        acc_ref[...] += jnp.dot(act, wd_ref[...], preferred_element_type=F32)

        @pl.when(f == pl.num_programs(1) - 1)
        def _():
            if gated_residual:
                o_ref[...] = res_ref[...] + gate_ref[...] * acc_ref[...]
            else:
                o_ref[...] = acc_ref[...]

    @pl.when(jnp.logical_and(i >= nact_ref[0], f == 0))
    def _():
        o_ref[...] = jnp.zeros(o_ref.shape, o_ref.dtype)


def _ffn(x, wg, wu, wd, eid, nact, *, tm, res=None, gate=None, seq=None):
    m, d = x.shape
    n_f = wg.shape[2]
    tf = _tile(n_f, 512, LANES)
    nf = n_f // tf
    gated = res is not None

    def row(i, nact_r):
        return jnp.minimum(i, nact_r[0] - 1)

    def fblk(i, f, nact_r):
        return jnp.where(i < nact_r[0], f, nf - 1)

    in_specs = [
        pl.BlockSpec((tm, d), lambda i, f, e, n: (row(i, n), 0)),
        pl.BlockSpec((None, d, tf), lambda i, f, e, n: (e[row(i, n)], 0, fblk(i, f, n))),
        pl.BlockSpec((None, d, tf), lambda i, f, e, n: (e[row(i, n)], 0, fblk(i, f, n))),
        pl.BlockSpec((None, tf, d), lambda i, f, e, n: (e[row(i, n)], fblk(i, f, n), 0)),
    ]
    args = [x, wg, wu, wd]
    if gated:
        per = seq // tm
        nb = gate.shape[0]
        in_specs.append(pl.BlockSpec((tm, d), lambda i, f, e, n: (i, 0)))
        in_specs.append(pl.BlockSpec(
            (None, 1, d),
            (lambda i, f, e, n: (i // per, 0, 0)) if nb > 1 else (lambda i, f, e, n: (0, 0, 0))))
        args += [res, gate]
    return pl.pallas_call(
        functools.partial(_ffn_kernel, gated_residual=gated),
        grid_spec=pltpu.PrefetchScalarGridSpec(
            num_scalar_prefetch=2,
            grid=(m // tm, nf),
            in_specs=in_specs,
            out_specs=pl.BlockSpec((tm, d), lambda i, f, e, n: (i, 0)),
            scratch_shapes=[pltpu.VMEM((tm, d), F32)],
        ),
        out_shape=jax.ShapeDtypeStruct((m, d), F32),
        compiler_params=_params("arbitrary", "arbitrary"),
        name="ffn",
    )(eid, nact, *args)


def _split_bf16(x):
    bits = lax.bitcast_convert_type(x, jnp.uint32) & jnp.uint32(0xFFFF0000)
    hi = lax.bitcast_convert_type(bits, F32)
    return hi.astype(BF16), (x - hi).astype(BF16)


def _route_kernel(x_ref, g_ref, sh_ref, sc_ref, wr_ref, h_ref, route_ref, cnt_ref, run_s,
                  *, n_experts):
    i = pl.program_id(0)

    @pl.when(i == 0)
    def _():
        run_s[...] = jnp.zeros(run_s.shape, run_s.dtype)

    h = _rms(x_ref[...], g_ref[...]) * (1.0 + sc_ref[...]) + sh_ref[...]
    h_ref[...] = h
    tm = h.shape[0]

    h_hi, h_lo = _split_bf16(h)
    w_hi, w_lo = _split_bf16(wr_ref[...])
    logits = (jnp.dot(h_hi, w_hi, preferred_element_type=F32)
              + jnp.dot(h_lo, w_hi, preferred_element_type=F32)
              + jnp.dot(h_hi, w_lo, preferred_element_type=F32)
              + jnp.dot(h_lo, w_lo, preferred_element_type=F32))

    lane = lax.broadcasted_iota(jnp.int32, logits.shape, 1)
    neg_inf = jnp.float32(-jnp.inf)
    lane_f = lane.astype(F32)
    masked = jnp.where(lane < n_experts, logits, neg_inf)
    m1 = jnp.max(masked, axis=-1, keepdims=True)
    i1 = jnp.min(jnp.where(masked == m1, lane_f, float(LANES)), axis=-1, keepdims=True)
    masked2 = jnp.where(lane_f == i1, neg_inf, masked)
    m2 = jnp.max(masked2, axis=-1, keepdims=True)
    i2 = jnp.min(jnp.where(masked2 == m2, lane_f, float(LANES)), axis=-1, keepdims=True)
    d = jnp.exp(m2 - m1)
    w1 = 1.0 / (1.0 + d)
    w2 = d / (1.0 + d)

    sel1 = lane_f == i1
    sel2 = lane_f == i2
    onehot = jnp.where(jnp.logical_or(sel1, sel2), 1.0, 0.0)
    r_i = lax.broadcasted_iota(jnp.int32, (tm, tm), 0)
    c_i = lax.broadcasted_iota(jnp.int32, (tm, tm), 1)
    tri = jnp.where(c_i < r_i, 1.0, 0.0).astype(BF16)
    rank = jnp.dot(tri, onehot.astype(BF16), preferred_element_type=F32) + run_s[...]
    r1 = jnp.sum(jnp.where(sel1, rank, 0.0), axis=-1, keepdims=True)
    r2 = jnp.sum(jnp.where(sel2, rank, 0.0), axis=-1, keepdims=True)
    run_s[...] = run_s[...] + jnp.sum(onehot, axis=0, keepdims=True)
    cnt_ref[...] = run_s[...]

    out = jnp.where(lane == 0, i1, 0.0)
    out = jnp.where(lane == 1, i2, out)
    out = jnp.where(lane == 2, r1, out)
    out = jnp.where(lane == 3, r2, out)
    out = jnp.where(lane == 4, w1, out)
    out = jnp.where(lane == 5, w2, out)
    route_ref[...] = out


def _route(x, g, shift, scale, router, seq):
    m, d = x.shape
    n_experts = router.shape[1]
    tm = _tile(seq, 256, SUBLANES)
    per = seq // tm
    wr = jnp.zeros((d, LANES), F32).at[:, :n_experts].set(router)
    bidx = (lambda i: (i // per, 0, 0)) if shift.shape[0] > 1 else (lambda i: (0, 0, 0))
    return pl.pallas_call(
        functools.partial(_route_kernel, n_experts=n_experts),
        grid=(m // tm,),
        in_specs=[
            pl.BlockSpec((tm, d), lambda i: (i, 0)),
            pl.BlockSpec((1, d), lambda i: (0, 0)),
            pl.BlockSpec((None, 1, d), bidx),
            pl.BlockSpec((None, 1, d), bidx),
            pl.BlockSpec((d, LANES), lambda i: (0, 0)),
        ],
        out_specs=[
            pl.BlockSpec((tm, d), lambda i: (i, 0)),
            pl.BlockSpec((tm, LANES), lambda i: (i, 0)),
            pl.BlockSpec((1, LANES), lambda i: (0, 0)),
        ],
        out_shape=[
            jax.ShapeDtypeStruct((m, d), F32),
            jax.ShapeDtypeStruct((m, LANES), F32),
            jax.ShapeDtypeStruct((1, LANES), F32),
        ],
        scratch_shapes=[pltpu.VMEM((1, LANES), F32)],
        compiler_params=_params("arbitrary"),
        name="route",
    )(x, g.reshape(1, d), shift, scale, wr)


def _dispatch_kernel(pos_ref, h_ref, xs_in_ref, xs_ref, sem, *, chunk):
    del xs_in_ref
    base = pl.program_id(0) * chunk

    def issue(t, carry):
        src = h_ref.at[pl.ds(t, 1)]
        for k in range(TOP_K):
            dst = xs_ref.at[pl.ds(pos_ref[(base + t) * TOP_K + k], 1)]
            pltpu.make_async_copy(src, dst, sem).start()
        return carry

    lax.fori_loop(0, chunk, issue, 0, unroll=DMA_ISSUE_UNROLL)
    for k in range(TOP_K):
        pltpu.make_async_copy(h_ref, xs_ref.at[pl.ds(0, chunk)], sem).wait()


def _dispatch(h, pos, rows_sorted):
    m, d = h.shape
    chunk = _tile(m, 256, SUBLANES)
    xs0 = jnp.zeros((rows_sorted, d), h.dtype)
    return pl.pallas_call(
        functools.partial(_dispatch_kernel, chunk=chunk),
        grid_spec=pltpu.PrefetchScalarGridSpec(
            num_scalar_prefetch=1,
            grid=(m // chunk,),
            in_specs=[pl.BlockSpec((chunk, d), lambda i, p: (i, 0)),
                      pl.BlockSpec(memory_space=pl.ANY)],
            out_specs=pl.BlockSpec(memory_space=pl.ANY),
            scratch_shapes=[pltpu.SemaphoreType.DMA(())],
        ),
        out_shape=jax.ShapeDtypeStruct((rows_sorted, d), h.dtype),
        input_output_aliases={2: 0},
        compiler_params=_params("arbitrary"),
        name="moe_dispatch",
    )(pos, h, xs0)


def _combine_kernel(pos_ref, ys_ref, lat_ref, route_ref, gate_ref, fg_ref, o_ref, ybuf, sem,
                    *, chunk, final_norm):
    base = pl.program_id(0) * chunk

    def issue(t, carry):
        for k in range(TOP_K):
            src = ys_ref.at[pl.ds(pos_ref[(base + t) * TOP_K + k], 1)]
            pltpu.make_async_copy(src, ybuf.at[k, pl.ds(t, 1)], sem).start()
        return carry

    lax.fori_loop(0, chunk, issue, 0, unroll=DMA_ISSUE_UNROLL)
    for k in range(TOP_K):
        pltpu.make_async_copy(ys_ref.at[pl.ds(0, chunk)], ybuf.at[k], sem).wait()

    route = route_ref[...]
    w1 = route[:, 4:5]
    w2 = route[:, 5:6]
    moe = w1 * ybuf[0] + w2 * ybuf[1]
    lat = lat_ref[...] + gate_ref[...] * moe
    o_ref[...] = _rms(lat, fg_ref[...]) if final_norm else lat


def _combine(ys, pos, lat, route, gate, final_g, seq, final_norm):
    m, d = lat.shape
    chunk = _tile(seq, 256, SUBLANES)
    per = seq // chunk
    nb = gate.shape[0]
    gidx = (lambda i, p: (i // per, 0, 0)) if nb > 1 else (lambda i, p: (0, 0, 0))
    return pl.pallas_call(
        functools.partial(_combine_kernel, chunk=chunk, final_norm=final_norm),
        grid_spec=pltpu.PrefetchScalarGridSpec(
            num_scalar_prefetch=1,
            grid=(m // chunk,),
            in_specs=[
                pl.BlockSpec(memory_space=pl.ANY),
                pl.BlockSpec((chunk, d), lambda i, p: (i, 0)),
                pl.BlockSpec((chunk, LANES), lambda i, p: (i, 0)),
                pl.BlockSpec((None, 1, d), gidx),
                pl.BlockSpec((1, d), lambda i, p: (0, 0)),
            ],
            out_specs=pl.BlockSpec((chunk, d), lambda i, p: (i, 0)),
            scratch_shapes=[pltpu.VMEM((TOP_K, chunk, d), F32), pltpu.SemaphoreType.DMA(())],
        ),
        out_shape=jax.ShapeDtypeStruct((m, d), F32),
        compiler_params=_params("arbitrary"),
        name="moe_combine",
    )(pos, ys, lat, route, gate, final_g.reshape(1, d))


def _final_norm_kernel(x_ref, g_ref, o_ref):
    o_ref[...] = _rms(x_ref[...], g_ref[...])


def _final_norm(x, g):
    m, d = x.shape
    tm = _tile(m, 256, SUBLANES)
    return pl.pallas_call(
        _final_norm_kernel,
        grid=(m // tm,),
        in_specs=[pl.BlockSpec((tm, d), lambda i: (i, 0)), pl.BlockSpec((1, d), lambda i: (0, 0))],
        out_specs=pl.BlockSpec((tm, d), lambda i: (i, 0)),
        out_shape=jax.ShapeDtypeStruct((m, d), F32),
        compiler_params=_params("arbitrary"),
        name="final_norm",
    )(x, g.reshape(1, d))


MOE_TILE_ROWS = 512
FFN_TILE_ROWS = 512


def _mixer_proj(rows, mods, norm_g, w_in_l, seq):
    h = _norm_mod(rows, norm_g, mods[0], mods[1], seq)
    return _matmul(h, w_in_l, F32)


def _mixer_out(rows, proj, z, mods, w_fo_l, w_ro_l, w_o_l, seq, d_f, d_lru, d):
    uc, us = _fourier_ch(proj, d_f)
    y = _fourier_seq(uc, us, seq)
    merged = _merge(y, z, w_fo_l, w_ro_l, proj, d_f + 2 * d_lru, d_f + 2 * d_lru + d)
    return _matmul_res(merged, w_o_l, rows, mods[2], seq)


def _dense_ffn(rows, mods, norm_g, wg, wu, wd, seq):
    m = rows.shape[0]
    h = _norm_mod(rows, norm_g, mods[3], mods[4], seq)
    tm = _tile(seq, FFN_TILE_ROWS, SUBLANES)
    eid = jnp.zeros((m // tm,), jnp.int32)
    nact = jnp.full((1,), m // tm, jnp.int32)
    return _ffn(h, wg, wu, wd, eid, nact, tm=tm, res=rows, gate=mods[5], seq=seq)


def _moe_ffn(rows, mods, norm_g, router, wg, wu, wd, final_g, seq, final_norm):
    m, d = rows.shape
    n_experts = router.shape[1]
    tm = _tile(m * TOP_K, MOE_TILE_ROWS, SUBLANES)
    h, route, counts = _route(rows, norm_g, mods[3], mods[4], router, seq)
    expert = route[:, 0:TOP_K].astype(jnp.int32)
    rank = route[:, TOP_K:2 * TOP_K].astype(jnp.int32)
    counts = counts[0, :n_experts].astype(jnp.int32)
    padded = ((counts + tm - 1) // tm) * tm
    ends = jnp.cumsum(padded)
    starts = ends - padded
    pos = (starts[expert] + rank).reshape(-1)
    n_tiles = (m * TOP_K) // tm + n_experts
    tile_start = jnp.arange(n_tiles, dtype=jnp.int32) * tm
    eid = jnp.minimum(jnp.sum(tile_start[:, None] >= ends[None, :], axis=1), n_experts - 1)
    nact = (ends[-1:] // tm).astype(jnp.int32)
    xs = _dispatch(h, pos, n_tiles * tm)
    ys = _ffn(xs, wg, wu, wd, eid.astype(jnp.int32), nact, tm=tm)
    return _combine(ys, pos, rows, route, mods[5], final_g, seq, final_norm)


def kernel(x, c, ctx, c_ctx, ada_w, ada_b, norm1_g, norm2_g, w_in, conv_w, conv_b, lru_wa,
           lru_ba, lru_wx, lru_bx, lru_lambda, w_fourier_out, w_lru_out, w_out, ffn_w_gate,
           ffn_w_up, ffn_w_down, moe_router, moe_w_gate, moe_w_up, moe_w_down, final_norm_g):
    nb, seq, d = x.shape
    ctx_len = ctx.shape[1]
    depth = ada_w.shape[0]
    d_lru = lru_lambda.shape[-1]
    d_f = w_fourier_out.shape[1]

    lat = x.reshape(nb * seq, d)
    cx = ctx.reshape(nb * ctx_len, d)

    pad = (-(nb + 1)) % SUBLANES
    cc = jnp.concatenate([c, c_ctx[None], jnp.zeros((pad, d), F32)], axis=0)
    mods_all = _ada_mod(cc, ada_w, ada_b)

    bf = lambda w: w.astype(BF16)
    out = None
    for l in range(depth):
        last = l == depth - 1
        m6 = mods_all[l].reshape(-1, 6, d)
        mods_lat = [m6[:nb, k].reshape(nb, 1, d) for k in range(6)]
        mods_ctx = [m6[nb:nb + 1, k].reshape(1, 1, d) for k in range(6)]
        w_in_l = bf(w_in[l])
        lru_p = (conv_w[l], conv_b[l], lru_wa[l], lru_ba[l], lru_wx[l], lru_bx[l], lru_lambda[l])
        zeros = jnp.zeros((nb, d_lru), F32)

        proj_c = _mixer_proj(cx, mods_ctx, norm1_g[l], w_in_l, ctx_len)
        z_c, htf, htb = _lru(proj_c, d_f, d_f + d_lru, *lru_p, zeros, zeros,
                             seq=ctx_len, conv_row=ctx_len, want_z=not last)
        w_fo_l, w_ro_l, w_o_l = bf(w_fourier_out[l]), bf(w_lru_out[l]), bf(w_out[l])
        if not last:
            cx = _mixer_out(cx, proj_c, z_c, mods_ctx, w_fo_l, w_ro_l, w_o_l,
                            ctx_len, d_f, d_lru, d)

        proj_l = _mixer_proj(lat, mods_lat, norm1_g[l], w_in_l, seq)
        z_l, _, _ = _lru(proj_l, d_f, d_f + d_lru, *lru_p, htf, htb,
                         seq=seq, conv_row=GRID_W, want_z=True)
        lat = _mixer_out(lat, proj_l, z_l, mods_lat, w_fo_l, w_ro_l, w_o_l, seq, d_f, d_lru, d)

        j = l // 2
        if l % 2 == 0:
            wg, wu, wd = bf(ffn_w_gate[j:j + 1]), bf(ffn_w_up[j:j + 1]), bf(ffn_w_down[j:j + 1])
            if not last:
                cx = _dense_ffn(cx, mods_ctx, norm2_g[l], wg, wu, wd, ctx_len)
            lat = _dense_ffn(lat, mods_lat, norm2_g[l], wg, wu, wd, seq)
            if last:
                out = _final_norm(lat, final_norm_g)
        else:
            wg, wu, wd = bf(moe_w_gate[j]), bf(moe_w_up[j]), bf(moe_w_down[j])
            if not last:
                cx = _moe_ffn(cx, mods_ctx, norm2_g[l], moe_router[j], wg, wu, wd,
                              final_norm_g, ctx_len, False)
            lat = _moe_ffn(lat, mods_lat, norm2_g[l], moe_router[j], wg, wu, wd,
                           final_norm_g, seq, last)
            out = lat
    return out.reshape(nb, seq, d)
```

```python
import functools

import numpy as np
import jax
import jax.numpy as jnp
from jax import lax
from jax.experimental import pallas as pl
from jax.experimental.pallas import tpu as pltpu

F32 = jnp.float32
BF16 = jnp.bfloat16

GRID_W = 64
FOURIER_GROUPS = 4
LRU_HEAD_DIM = 128
CONV_LEFT = 2
CONV_WIDTH = 4
LRU_C = 8.0
SCAN_UNROLL = 8
DMA_ISSUE_UNROLL = 8
TOP_K = 2
NORM_EPS = 1e-6

VMEM_LIMIT_BYTES = 56 * 1024 * 1024
LANES = 128
SUBLANES = 8


def _params(*sem):
    return pltpu.CompilerParams(dimension_semantics=sem, vmem_limit_bytes=VMEM_LIMIT_BYTES)


def _tile(dim, pref, align):
    if dim <= pref:
        return dim
    t = (pref // align) * align
    while t >= align:
        if dim % t == 0:
            return t
        t -= align
    return dim


def _ada_kernel(c_ref, w_ref, b_ref, o_ref):
    c = c_ref[...]
    a = (c * jax.nn.sigmoid(c)).astype(BF16)
    o_ref[...] = jnp.dot(a, w_ref[...].astype(BF16), preferred_element_type=F32) + b_ref[...]


def _ada_mod(cc, ada_w, ada_b):
    depth, d, n = ada_w.shape
    rows = cc.shape[0]
    tn = _tile(n, 1024, LANES)
    return pl.pallas_call(
        _ada_kernel,
        grid=(depth, n // tn),
        in_specs=[
            pl.BlockSpec((rows, d), lambda l, j: (0, 0)),
            pl.BlockSpec((None, d, tn), lambda l, j: (l, 0, j)),
            pl.BlockSpec((None, 1, tn), lambda l, j: (l, 0, j)),
        ],
        out_specs=pl.BlockSpec((None, rows, tn), lambda l, j: (l, 0, j)),
        out_shape=jax.ShapeDtypeStruct((depth, rows, n), F32),
        compiler_params=_params("arbitrary", "arbitrary"),
        name="ada_mod",
    )(cc, ada_w, ada_b.reshape(depth, 1, n))


def _rms(x, g):
    return x * lax.rsqrt(jnp.mean(x * x, axis=-1, keepdims=True) + NORM_EPS) * g


def _sigmoid(x):
    return 0.5 * jnp.tanh(0.5 * x) + 0.5


def _norm_mod_value(x_ref, g_ref, sh_ref, sc_ref):
    return _rms(x_ref[...], g_ref[...]) * (1.0 + sc_ref[...]) + sh_ref[...]


def _block_major(w, tn):
    *lead, k, n = w.shape
    w = w.reshape(*lead, k, n // tn, tn)
    return jnp.swapaxes(w, -3, -2)


def _norm_mm_kernel(x_ref, g_ref, sh_ref, sc_ref, w_ref, o_ref, h_s):
    @pl.when(pl.program_id(1) == 0)
    def _():
        h_s[...] = _norm_mod_value(x_ref, g_ref, sh_ref, sc_ref).astype(h_s.dtype)

    o_ref[...] = jnp.dot(h_s[...], w_ref[...], preferred_element_type=F32).astype(o_ref.dtype)


def _norm_matmul(x, g, shift, scale, w, seq, out_dtype):
    m, d = x.shape
    tn = _tile(w.shape[1], W_IN_TILE_N, LANES)
    n_blocks = w.shape[1] // tn
    nb = shift.shape[0]
    tm = _tile(seq if nb > 1 else m, 1024, SUBLANES)
    per = seq // tm if nb > 1 else 1
    bidx = (lambda i, j: (i // per, 0, 0)) if nb > 1 else (lambda i, j: (0, 0, 0))
    return pl.pallas_call(
        _norm_mm_kernel,
        grid=(m // tm, n_blocks),
        in_specs=[
            pl.BlockSpec((tm, d), lambda i, j: (i, 0)),
            pl.BlockSpec((1, d), lambda i, j: (0, 0)),
            pl.BlockSpec((None, 1, d), bidx),
            pl.BlockSpec((None, 1, d), bidx),
            pl.BlockSpec((d, tn), lambda i, j: (0, j)),
        ],
        out_specs=pl.BlockSpec((tm, tn), lambda i, j: (i, j)),
        out_shape=jax.ShapeDtypeStruct((m, n_blocks * tn), out_dtype),
        scratch_shapes=[pltpu.VMEM((tm, d), BF16)],
        compiler_params=_params("arbitrary", "arbitrary"),
        name="norm_matmul",
    )(x, g.reshape(1, d), shift, scale, w)


def _dft_cos_sin(n):
    j = jnp.arange(n, dtype=jnp.int32)
    ang = ((j[:, None] * j[None, :]) % n).astype(F32) * (2.0 * np.pi / n)
    return jnp.cos(ang), jnp.sin(ang)


def _fourier_ch_kernel(a_ref, w_ref, uc_ref, us_ref):
    gs = uc_ref.shape[-1]
    z = jnp.dot(a_ref[...].astype(BF16), w_ref[...], preferred_element_type=F32)
    uc_ref[...] = z[:, :gs].astype(uc_ref.dtype)
    us_ref[...] = z[:, gs:].astype(us_ref.dtype)


def _fourier_ch(proj, d_f):
    m = proj.shape[0]
    gs = d_f // FOURIER_GROUPS
    c, s = _dft_cos_sin(gs)
    w = jnp.concatenate([c, s], axis=1).astype(BF16)
    tm = _tile(m, 1024, SUBLANES)
    out = jax.ShapeDtypeStruct((m, d_f), BF16)
    return pl.pallas_call(
        _fourier_ch_kernel,
        grid=(m // tm, FOURIER_GROUPS),
        in_specs=[
            pl.BlockSpec((tm, gs), lambda i, g: (i, g)),
            pl.BlockSpec((gs, 2 * gs), lambda i, g: (0, 0)),
        ],
        out_specs=[pl.BlockSpec((tm, gs), lambda i, g: (i, g)),
                   pl.BlockSpec((tm, gs), lambda i, g: (i, g))],
        out_shape=[out, out],
        compiler_params=_params("arbitrary", "arbitrary"),
        name="fourier_ch",
    )(proj, w)


def _fourier_seq_kernel(c_ref, s_ref, uc_ref, us_ref, o_ref, *, scale):
    acc = jnp.dot(c_ref[...], uc_ref[...], preferred_element_type=F32)
    acc = acc - jnp.dot(s_ref[...], us_ref[...], preferred_element_type=F32)
    o_ref[...] = (acc * scale).astype(o_ref.dtype)


def _fourier_seq(uc, us, seq):
    m, d_f = uc.shape
    nb = m // seq
    gs = d_f // FOURIER_GROUPS
    c, s = _dft_cos_sin(seq)
    tm = _tile(seq, 1024, SUBLANES)
    tn = _tile(d_f, 512, LANES)
    per = seq // tm
    scale = float(1.0 / np.sqrt(float(seq) * float(gs)))
    return pl.pallas_call(
        functools.partial(_fourier_seq_kernel, scale=scale),
        grid=(nb, per, d_f // tn),
        in_specs=[
            pl.BlockSpec((tm, seq), lambda b, i, j: (i, 0)),
            pl.BlockSpec((tm, seq), lambda b, i, j: (i, 0)),
            pl.BlockSpec((seq, tn), lambda b, i, j: (b, j)),
            pl.BlockSpec((seq, tn), lambda b, i, j: (b, j)),
        ],
        out_specs=pl.BlockSpec((tm, tn), lambda b, i, j: (b * per + i, j)),
        out_shape=jax.ShapeDtypeStruct((m, d_f), BF16),
        compiler_params=_params("arbitrary", "arbitrary", "arbitrary"),
        name="fourier_seq",
    )(c.astype(BF16), s.astype(BF16), uc, us)


def _lru_kernel(ur_ref, ug_ref, cw_ref, cb_ref, wg_ref, bg_ref, lam_ref, h0f_ref, h0b_ref,
                z_ref, htf_ref, htb_ref,
                af_s, bf_s, ab_s, bb_s, rf_s, rb_s, *, seq, conv_w, want_z):
    hd = LRU_HEAD_DIM
    seg = seq // SUBLANES
    pitch = seg + SUBLANES

    u = ur_ref[...].astype(F32)
    t = lax.broadcasted_iota(jnp.int32, (seq, hd), 0)
    col = t % conv_w

    def tap(k):
        rolled = pltpu.roll(u, shift=(-k) % seq, axis=0)
        ok = jnp.logical_and(col + k >= 0, col + k < conv_w)
        return jnp.where(ok, rolled, 0.0)

    cw = cw_ref[...]
    v = tap(-CONV_LEFT) * cw[0:1]
    for k in range(1, CONV_WIDTH):
        src = u if k == CONV_LEFT else tap(k - CONV_LEFT)
        v = v + src * cw[k:k + 1]
    v = v + cb_ref[...]

    gates = jnp.dot(v.astype(BF16), wg_ref[...], preferred_element_type=F32)
    bg = bg_ref[...]
    lam = lam_ref[...]
    neg = -lam
    softplus = jnp.maximum(neg, 0.0) + jnp.log1p(jnp.exp(-jnp.abs(neg)))

    def coeffs(d):
        r = _sigmoid(gates[:, (2 * d) * hd:(2 * d + 1) * hd] + bg[2 * d:2 * d + 1])
        i = _sigmoid(gates[:, (2 * d + 1) * hd:(2 * d + 2) * hd] + bg[2 * d + 1:2 * d + 2])
        log_a = (-LRU_C) * r * softplus[d:d + 1]
        a = jnp.exp(log_a)
        y = 1.0 - a * a
        b = (y * lax.rsqrt(jnp.maximum(y, 1e-37))) * (i * v)
        return a, b

    a_f, b_f = coeffs(0)
    a_b, b_b = coeffs(1)
    for s in range(SUBLANES):
        src = slice(s * seg, (s + 1) * seg)
        dst = slice(s * pitch, s * pitch + seg)
        af_s[dst, :] = a_f[src]
        bf_s[dst, :] = b_f[src]
        ab_s[dst, :] = a_b[src]
        bb_s[dst, :] = b_b[src]

    def rows(ref, j):
        return ref[pl.ds(j, SUBLANES, stride=pitch), :]

    zero = jnp.zeros((SUBLANES, hd), F32)
    one = jnp.ones((SUBLANES, hd), F32)

    def totals(j, carry):
        hf, pf, hb, pb = carry
        jb = seg - 1 - j
        a = rows(af_s, j)
        hf = a * hf + rows(bf_s, j)
        pf = a * pf
        a = rows(ab_s, jb)
        hb = a * hb + rows(bb_s, jb)
        pb = a * pb
        return hf, pf, hb, pb

    hf, pf, hb, pb = lax.fori_loop(0, seg, totals, (zero, one, zero, one), unroll=SCAN_UNROLL)

    cf = [h0f_ref[...]]
    for s in range(1, SUBLANES):
        cf.append(pf[s - 1:s] * cf[-1] + hf[s - 1:s])
    cb = [h0b_ref[...]]
    for s in range(SUBLANES - 2, -1, -1):
        cb.append(pb[s + 1:s + 2] * cb[-1] + hb[s + 1:s + 2])
    cb = cb[::-1]
    htf_ref[...] = pf[SUBLANES - 1:SUBLANES] * cf[-1] + hf[SUBLANES - 1:SUBLANES]
    htb_ref[...] = pb[0:1] * cb[0] + hb[0:1]

    if want_z:
        cf = jnp.concatenate(cf, axis=0)
        cb = jnp.concatenate(cb, axis=0)

        def states(j, carry):
            hf, hb = carry
            jb = seg - 1 - j
            hf = rows(af_s, j) * hf + rows(bf_s, j)
            rf_s[pl.ds(j, SUBLANES, stride=pitch), :] = hf
            hb = rows(ab_s, jb) * hb + rows(bb_s, jb)
            rb_s[pl.ds(jb, SUBLANES, stride=pitch), :] = hb
            return hf, hb

        lax.fori_loop(0, seg, states, (cf, cb), unroll=SCAN_UNROLL)

        gate = jax.nn.gelu(ug_ref[...].astype(F32))
        for s in range(SUBLANES):
            src = slice(s * seg, (s + 1) * seg)
            dst = slice(s * pitch, s * pitch + seg)
            z_ref[src, :] = ((rf_s[dst, :] + rb_s[dst, :]) * gate[src]).astype(z_ref.dtype)
    else:
        z_ref[...] = jnp.zeros(z_ref.shape, z_ref.dtype)


def _lru(proj, ur_col0, ug_col0, conv_w_l, conv_b_l, wa, ba, wx, bx, lam, h0f, h0b, *,
         seq, conv_row, want_z):
    m = proj.shape[0]
    nb = m // seq
    hd = LRU_HEAD_DIM
    heads = wa.shape[1]
    d_lru = heads * hd
    seg = seq // SUBLANES
    pitch = seg + SUBLANES
    wg = jnp.concatenate([wa[0], wx[0], wa[1], wx[1]], axis=-1).astype(BF16)
    bg = jnp.stack([ba[0], bx[0], ba[1], bx[1]], axis=0)
    state = jax.ShapeDtypeStruct((nb, 1, d_lru), F32)
    cblk = lambda c0: (lambda b, h: (b, c0 // hd + h))
    z, htf, htb = pl.pallas_call(
        functools.partial(_lru_kernel, seq=seq, conv_w=conv_row, want_z=want_z),
        grid=(nb, heads),
        in_specs=[
            pl.BlockSpec((seq, hd), cblk(ur_col0)),
            pl.BlockSpec((seq, hd), cblk(ug_col0)),
            pl.BlockSpec((CONV_WIDTH, hd), lambda b, h: (0, h)),
            pl.BlockSpec((1, hd), lambda b, h: (0, h)),
            pl.BlockSpec((None, hd, 4 * hd), lambda b, h: (h, 0, 0)),
            pl.BlockSpec((4, hd), lambda b, h: (0, h)),
            pl.BlockSpec((2, hd), lambda b, h: (0, h)),
            pl.BlockSpec((None, 1, hd), lambda b, h: (b, 0, h)),
            pl.BlockSpec((None, 1, hd), lambda b, h: (b, 0, h)),
        ],
        out_specs=[
            pl.BlockSpec((seq, hd), lambda b, h: (b, h)),
            pl.BlockSpec((None, 1, hd), lambda b, h: (b, 0, h)),
            pl.BlockSpec((None, 1, hd), lambda b, h: (b, 0, h)),
        ],
        out_shape=[jax.ShapeDtypeStruct((m, d_lru), BF16), state, state],
        scratch_shapes=[pltpu.VMEM((SUBLANES * pitch, hd), F32) for _ in range(6)],
        compiler_params=_params("arbitrary", "arbitrary"),
        name="lru",
    )(proj, proj, conv_w_l, conv_b_l.reshape(1, d_lru), wg, bg, lam,
      h0f.reshape(nb, 1, d_lru), h0b.reshape(nb, 1, d_lru))
    return z, htf.reshape(nb, d_lru), htb.reshape(nb, d_lru)


def _merge_kernel(y_ref, z_ref, wf_ref, wr_ref, gf_ref, gr_ref, o_ref):
    yf = jnp.dot(y_ref[...], wf_ref[...], preferred_element_type=F32)
    yr = jnp.dot(z_ref[...], wr_ref[...], preferred_element_type=F32)
    o_ref[...] = (_sigmoid(gf_ref[...].astype(F32)) * yf
                  + _sigmoid(gr_ref[...].astype(F32)) * yr).astype(o_ref.dtype)


def _merge(y, z, w_fo, w_ro, proj, gf_col0, gr_col0):
    m, d_f = y.shape
    d_lru = z.shape[1]
    d = w_fo.shape[1]
    tm = _tile(m, 1024, SUBLANES)
    tn = _tile(int(np.gcd(np.gcd(gf_col0, gr_col0), d)), 512, LANES)
    return pl.pallas_call(
        _merge_kernel,
        grid=(m // tm, d // tn),
        in_specs=[
            pl.BlockSpec((tm, d_f), lambda i, j: (i, 0)),
            pl.BlockSpec((tm, d_lru), lambda i, j: (i, 0)),
            pl.BlockSpec((d_f, tn), lambda i, j: (0, j)),
            pl.BlockSpec((d_lru, tn), lambda i, j: (0, j)),
            pl.BlockSpec((tm, tn), lambda i, j: (i, gf_col0 // tn + j)),
            pl.BlockSpec((tm, tn), lambda i, j: (i, gr_col0 // tn + j)),
        ],
        out_specs=pl.BlockSpec((tm, tn), lambda i, j: (i, j)),
        out_shape=jax.ShapeDtypeStruct((m, d), BF16),
        compiler_params=_params("arbitrary", "arbitrary"),
        name="merge",
    )(y, z, w_fo, w_ro, proj, proj)


def _mm_res_kernel(a_ref, w_ref, res_ref, gate_ref, o_ref):
    acc = jnp.dot(a_ref[...], w_ref[...], preferred_element_type=F32)
    o_ref[...] = res_ref[...] + gate_ref[...] * acc


def _matmul_res(a, w, res, gate, seq):
    m, k = a.shape
    n = w.shape[1]
    tm = _tile(seq, 1024, SUBLANES)
    tn = _tile(n, 1024, LANES)
    per = seq // tm
    nb = gate.shape[0]
    gidx = (lambda i, j: (i // per, 0, j)) if nb > 1 else (lambda i, j: (0, 0, j))
    return pl.pallas_call(
        _mm_res_kernel,
        grid=(m // tm, n // tn),
        in_specs=[
            pl.BlockSpec((tm, k), lambda i, j: (i, 0)),
            pl.BlockSpec((k, tn), lambda i, j: (0, j)),
            pl.BlockSpec((tm, tn), lambda i, j: (i, j)),
            pl.BlockSpec((None, 1, tn), gidx),
        ],
        out_specs=pl.BlockSpec((tm, tn), lambda i, j: (i, j)),
        out_shape=jax.ShapeDtypeStruct((m, n), F32),
        compiler_params=_params("arbitrary", "arbitrary"),
        name="matmul_res",
    )(a, w, res, gate)


def _ffn_kernel(eid_ref, nact_ref, x_ref, wg_ref, wu_ref, wd_ref, *rest, modulated):
    if modulated:
        g_ref, sh_ref, sc_ref, gate_ref, o_ref, xb_s, acc_ref = rest
    else:
        o_ref, xb_s, acc_ref = rest
    i = pl.program_id(0)
    f = pl.program_id(1)

    @pl.when(i < nact_ref[0])
    def _():
        @pl.when(f == 0)
        def _():
            h = _norm_mod_value(x_ref, g_ref, sh_ref, sc_ref) if modulated else x_ref[...]
            xb_s[...] = h.astype(xb_s.dtype)
            acc_ref[...] = jnp.zeros(acc_ref.shape, acc_ref.dtype)

        x = xb_s[...]
        g = jnp.dot(x, wg_ref[...], preferred_element_type=F32)
        u = jnp.dot(x, wu_ref[...], preferred_element_type=F32)
        act = ((g * _sigmoid(g)) * u).astype(BF16)
        acc_ref[...] += jnp.dot(act, wd_ref[...], preferred_element_type=F32)

        @pl.when(f == pl.num_programs(1) - 1)
        def _():
            if modulated:
                o_ref[...] = x_ref[...] + gate_ref[...] * acc_ref[...]
            else:
                o_ref[...] = acc_ref[...]

    @pl.when(jnp.logical_and(i >= nact_ref[0], f == 0))
    def _():
        o_ref[...] = jnp.zeros(o_ref.shape, o_ref.dtype)


def _ffn(x, wg_blocks, wu_blocks, wd, eid, nact, *, tm, mod=None, seq=None):
    m, d = x.shape
    nf, tf = wg_blocks.shape[1], wg_blocks.shape[3]
    modulated = mod is not None

    def row(i, nact_r):
        return jnp.minimum(i, nact_r[0] - 1)

    def fblk(i, f, nact_r):
        return jnp.where(i < nact_r[0], f, nf - 1)

    in_specs = [
        pl.BlockSpec((tm, d), lambda i, f, e, n: (row(i, n), 0)),
        pl.BlockSpec((None, None, d, tf), lambda i, f, e, n: (e[row(i, n)], fblk(i, f, n), 0, 0)),
        pl.BlockSpec((None, None, d, tf), lambda i, f, e, n: (e[row(i, n)], fblk(i, f, n), 0, 0)),
        pl.BlockSpec((None, tf, d), lambda i, f, e, n: (e[row(i, n)], fblk(i, f, n), 0)),
    ]
    args = [x, wg_blocks, wu_blocks, wd]
    if modulated:
        norm_g, shift, scale, gate = mod
        nb = gate.shape[0]
        per = seq // tm
        bidx = (lambda i, f, e, n: (i // per, 0, 0)) if nb > 1 else (lambda i, f, e, n: (0, 0, 0))
        in_specs.append(pl.BlockSpec((1, d), lambda i, f, e, n: (0, 0)))
        in_specs += [pl.BlockSpec((None, 1, d), bidx) for _ in range(3)]
        args += [norm_g.reshape(1, d), shift, scale, gate]
    return pl.pallas_call(
        functools.partial(_ffn_kernel, modulated=modulated),
        grid_spec=pltpu.PrefetchScalarGridSpec(
            num_scalar_prefetch=2,
            grid=(m // tm, nf),
            in_specs=in_specs,
            out_specs=pl.BlockSpec((tm, d), lambda i, f, e, n: (i, 0)),
            scratch_shapes=[pltpu.VMEM((tm, d), BF16), pltpu.VMEM((tm, d), F32)],
        ),
        out_shape=jax.ShapeDtypeStruct((m, d), F32),
        compiler_params=_params("arbitrary", "arbitrary"),
        name="ffn",
    )(eid, nact, *args)


def _split_bf16(x):
    bits = lax.bitcast_convert_type(x, jnp.uint32) & jnp.uint32(0xFFFF0000)
    hi = lax.bitcast_convert_type(bits, F32)
    return hi.astype(BF16), (x - hi).astype(BF16)


def _route_kernel(x_ref, g_ref, sh_ref, sc_ref, wr_ref, h_ref, route_ref, cnt_ref, run_s,
                  *, n_experts):
    i = pl.program_id(0)

    @pl.when(i == 0)
    def _():
        run_s[...] = jnp.zeros(run_s.shape, run_s.dtype)

    h = _rms(x_ref[...], g_ref[...]) * (1.0 + sc_ref[...]) + sh_ref[...]
    h_ref[...] = h
    tm = h.shape[0]

    h_hi, h_lo = _split_bf16(h)
    w_hi, w_lo = _split_bf16(wr_ref[...])
    logits = (jnp.dot(h_hi, w_hi, preferred_element_type=F32)
              + jnp.dot(h_lo, w_hi, preferred_element_type=F32)
              + jnp.dot(h_hi, w_lo, preferred_element_type=F32)
              + jnp.dot(h_lo, w_lo, preferred_element_type=F32))

    lane = lax.broadcasted_iota(jnp.int32, logits.shape, 1)
    neg_inf = jnp.float32(-jnp.inf)
    lane_f = lane.astype(F32)
    masked = jnp.where(lane < n_experts, logits, neg_inf)
    m1 = jnp.max(masked, axis=-1, keepdims=True)
    i1 = jnp.min(jnp.where(masked == m1, lane_f, float(LANES)), axis=-1, keepdims=True)
    masked2 = jnp.where(lane_f == i1, neg_inf, masked)
    m2 = jnp.max(masked2, axis=-1, keepdims=True)
    i2 = jnp.min(jnp.where(masked2 == m2, lane_f, float(LANES)), axis=-1, keepdims=True)
    d = jnp.exp(m2 - m1)
    w1 = 1.0 / (1.0 + d)
    w2 = d / (1.0 + d)

    sel1 = lane_f == i1
    sel2 = lane_f == i2
    onehot = jnp.where(jnp.logical_or(sel1, sel2), 1.0, 0.0)
    r_i = lax.broadcasted_iota(jnp.int32, (tm, tm), 0)
    c_i = lax.broadcasted_iota(jnp.int32, (tm, tm), 1)
    tri = jnp.where(c_i < r_i, 1.0, 0.0).astype(BF16)
    rank = jnp.dot(tri, onehot.astype(BF16), preferred_element_type=F32) + run_s[...]
    r1 = jnp.sum(jnp.where(sel1, rank, 0.0), axis=-1, keepdims=True)
    r2 = jnp.sum(jnp.where(sel2, rank, 0.0), axis=-1, keepdims=True)
    run_s[...] = run_s[...] + jnp.sum(onehot, axis=0, keepdims=True)
    cnt_ref[...] = run_s[...]

    out = jnp.where(lane == 0, i1, 0.0)
    out = jnp.where(lane == 1, i2, out)
    out = jnp.where(lane == 2, r1, out)
    out = jnp.where(lane == 3, r2, out)
    out = jnp.where(lane == 4, w1, out)
    out = jnp.where(lane == 5, w2, out)
    route_ref[...] = out


def _route(x, g, shift, scale, router, seq):
    m, d = x.shape
    n_experts = router.shape[1]
    tm = _tile(seq, 256, SUBLANES)
    per = seq // tm
    wr = jnp.zeros((d, LANES), F32).at[:, :n_experts].set(router)
    bidx = (lambda i: (i // per, 0, 0)) if shift.shape[0] > 1 else (lambda i: (0, 0, 0))
    return pl.pallas_call(
        functools.partial(_route_kernel, n_experts=n_experts),
        grid=(m // tm,),
        in_specs=[
            pl.BlockSpec((tm, d), lambda i: (i, 0)),
            pl.BlockSpec((1, d), lambda i: (0, 0)),
            pl.BlockSpec((None, 1, d), bidx),
            pl.BlockSpec((None, 1, d), bidx),
            pl.BlockSpec((d, LANES), lambda i: (0, 0)),
        ],
        out_specs=[
            pl.BlockSpec((tm, d), lambda i: (i, 0)),
            pl.BlockSpec((tm, LANES), lambda i: (i, 0)),
            pl.BlockSpec((1, LANES), lambda i: (0, 0)),
        ],
        out_shape=[
            jax.ShapeDtypeStruct((m, d), F32),
            jax.ShapeDtypeStruct((m, LANES), F32),
            jax.ShapeDtypeStruct((1, LANES), F32),
        ],
        scratch_shapes=[pltpu.VMEM((1, LANES), F32)],
        compiler_params=_params("arbitrary"),
        name="route",
    )(x, g.reshape(1, d), shift, scale, wr)


def _dispatch_kernel(pos_ref, h_ref, xs_in_ref, xs_ref, sem, *, chunk):
    del xs_in_ref
    base = pl.program_id(0) * chunk

    def issue(t, carry):
        src = h_ref.at[pl.ds(t, 1)]
        for k in range(TOP_K):
            dst = xs_ref.at[pl.ds(pos_ref[(base + t) * TOP_K + k], 1)]
            pltpu.make_async_copy(src, dst, sem).start()
        return carry

    lax.fori_loop(0, chunk, issue, 0, unroll=DMA_ISSUE_UNROLL)
    for k in range(TOP_K):
        pltpu.make_async_copy(h_ref, xs_ref.at[pl.ds(0, chunk)], sem).wait()


def _dispatch(h, pos, rows_sorted):
    m, d = h.shape
    chunk = _tile(m, 256, SUBLANES)
    xs0 = jnp.zeros((rows_sorted, d), h.dtype)
    return pl.pallas_call(
        functools.partial(_dispatch_kernel, chunk=chunk),
        grid_spec=pltpu.PrefetchScalarGridSpec(
            num_scalar_prefetch=1,
            grid=(m // chunk,),
            in_specs=[pl.BlockSpec((chunk, d), lambda i, p: (i, 0)),
                      pl.BlockSpec(memory_space=pl.ANY)],
            out_specs=pl.BlockSpec(memory_space=pl.ANY),
            scratch_shapes=[pltpu.SemaphoreType.DMA(())],
        ),
        out_shape=jax.ShapeDtypeStruct((rows_sorted, d), h.dtype),
        input_output_aliases={2: 0},
        compiler_params=_params("arbitrary"),
        name="moe_dispatch",
    )(pos, h, xs0)


def _combine_kernel(pos_ref, ys_ref, lat_ref, route_ref, gate_ref, fg_ref, o_ref, ybuf, sem,
                    *, chunk, final_norm):
    base = pl.program_id(0) * chunk

    def issue(t, carry):
        for k in range(TOP_K):
            src = ys_ref.at[pl.ds(pos_ref[(base + t) * TOP_K + k], 1)]
            pltpu.make_async_copy(src, ybuf.at[k, pl.ds(t, 1)], sem).start()
        return carry

    lax.fori_loop(0, chunk, issue, 0, unroll=DMA_ISSUE_UNROLL)
    for k in range(TOP_K):
        pltpu.make_async_copy(ys_ref.at[pl.ds(0, chunk)], ybuf.at[k], sem).wait()

    route = route_ref[...]
    w1 = route[:, 4:5]
    w2 = route[:, 5:6]
    moe = w1 * ybuf[0] + w2 * ybuf[1]
    lat = lat_ref[...] + gate_ref[...] * moe
    o_ref[...] = _rms(lat, fg_ref[...]) if final_norm else lat


def _combine(ys, pos, lat, route, gate, final_g, seq, final_norm):
    m, d = lat.shape
    chunk = _tile(seq, 256, SUBLANES)
    per = seq // chunk
    nb = gate.shape[0]
    gidx = (lambda i, p: (i // per, 0, 0)) if nb > 1 else (lambda i, p: (0, 0, 0))
    return pl.pallas_call(
        functools.partial(_combine_kernel, chunk=chunk, final_norm=final_norm),
        grid_spec=pltpu.PrefetchScalarGridSpec(
            num_scalar_prefetch=1,
            grid=(m // chunk,),
            in_specs=[
                pl.BlockSpec(memory_space=pl.ANY),
                pl.BlockSpec((chunk, d), lambda i, p: (i, 0)),
                pl.BlockSpec((chunk, LANES), lambda i, p: (i, 0)),
                pl.BlockSpec((None, 1, d), gidx),
                pl.BlockSpec((1, d), lambda i, p: (0, 0)),
            ],
            out_specs=pl.BlockSpec((chunk, d), lambda i, p: (i, 0)),
            scratch_shapes=[pltpu.VMEM((TOP_K, chunk, d), F32), pltpu.SemaphoreType.DMA(())],
        ),
        out_shape=jax.ShapeDtypeStruct((m, d), F32),
        compiler_params=_params("arbitrary"),
        name="moe_combine",
    )(pos, ys, lat, route, gate, final_g.reshape(1, d))


def _final_norm_kernel(x_ref, g_ref, o_ref):
    o_ref[...] = _rms(x_ref[...], g_ref[...])


def _final_norm(x, g):
    m, d = x.shape
    tm = _tile(m, 256, SUBLANES)
    return pl.pallas_call(
        _final_norm_kernel,
        grid=(m // tm,),
        in_specs=[pl.BlockSpec((tm, d), lambda i: (i, 0)), pl.BlockSpec((1, d), lambda i: (0, 0))],
        out_specs=pl.BlockSpec((tm, d), lambda i: (i, 0)),
        out_shape=jax.ShapeDtypeStruct((m, d), F32),
        compiler_params=_params("arbitrary"),
        name="final_norm",
    )(x, g.reshape(1, d))


MOE_TILE_ROWS = 512
FFN_TILE_ROWS = 512
MOE_TILE_F = 1024
FFN_TILE_F = 512
W_IN_TILE_N = 1024


def _ffn_weights(wg, wu, wd, tf_pref):
    tf = _tile(wg.shape[-1], tf_pref, LANES)
    return (_block_major(wg.astype(BF16), tf), _block_major(wu.astype(BF16), tf), wd.astype(BF16))


def _mixer_proj(rows, mods, norm_g, w_in_blocks, seq):
    return _norm_matmul(rows, norm_g, mods[0], mods[1], w_in_blocks, seq, BF16)


def _mixer_out(rows, proj, z, mods, w_fo_l, w_ro_l, w_o_l, seq, d_f, d_lru, d):
    uc, us = _fourier_ch(proj, d_f)
    y = _fourier_seq(uc, us, seq)
    merged = _merge(y, z, w_fo_l, w_ro_l, proj, d_f + 2 * d_lru, d_f + 2 * d_lru + d)
    return _matmul_res(merged, w_o_l, rows, mods[2], seq)


def _dense_ffn(rows, mods, norm_g, wg, wu, wd, seq):
    m = rows.shape[0]
    tm = _tile(seq, FFN_TILE_ROWS, SUBLANES)
    eid = jnp.zeros((m // tm,), jnp.int32)
    nact = jnp.full((1,), m // tm, jnp.int32)
    return _ffn(rows, wg, wu, wd, eid, nact, tm=tm,
                mod=(norm_g, mods[3], mods[4], mods[5]), seq=seq)


def _moe_ffn(rows, mods, norm_g, router, wg, wu, wd, final_g, seq, final_norm):
    m, d = rows.shape
    n_experts = router.shape[1]
    tm = _tile(m * TOP_K, MOE_TILE_ROWS, SUBLANES)
    h, route, counts = _route(rows, norm_g, mods[3], mods[4], router, seq)
    expert = route[:, 0:TOP_K].astype(jnp.int32)
    rank = route[:, TOP_K:2 * TOP_K].astype(jnp.int32)
    counts = counts[0, :n_experts].astype(jnp.int32)
    padded = ((counts + tm - 1) // tm) * tm
    ends = jnp.cumsum(padded)
    starts = ends - padded
    pos = (starts[expert] + rank).reshape(-1)
    n_tiles = (m * TOP_K) // tm + n_experts
    tile_start = jnp.arange(n_tiles, dtype=jnp.int32) * tm
    eid = jnp.minimum(jnp.sum(tile_start[:, None] >= ends[None, :], axis=1), n_experts - 1)
    nact = (ends[-1:] // tm).astype(jnp.int32)
    xs = _dispatch(h, pos, n_tiles * tm)
    ys = _ffn(xs, wg, wu, wd, eid.astype(jnp.int32), nact, tm=tm)
    return _combine(ys, pos, rows, route, mods[5], final_g, seq, final_norm)


def kernel(x, c, ctx, c_ctx, ada_w, ada_b, norm1_g, norm2_g, w_in, conv_w, conv_b, lru_wa,
           lru_ba, lru_wx, lru_bx, lru_lambda, w_fourier_out, w_lru_out, w_out, ffn_w_gate,
           ffn_w_up, ffn_w_down, moe_router, moe_w_gate, moe_w_up, moe_w_down, final_norm_g):
    nb, seq, d = x.shape
    ctx_len = ctx.shape[1]
    depth = ada_w.shape[0]
    d_lru = lru_lambda.shape[-1]
    d_f = w_fourier_out.shape[1]

    lat = x.reshape(nb * seq, d)
    cx = ctx.reshape(nb * ctx_len, d)

    pad = (-(nb + 1)) % SUBLANES
    cc = jnp.concatenate([c, c_ctx[None], jnp.zeros((pad, d), F32)], axis=0)
    mods_all = _ada_mod(cc, ada_w, ada_b)

    bf = lambda w: w.astype(BF16)
    out = None
    for l in range(depth):
        last = l == depth - 1
        m6 = mods_all[l].reshape(-1, 6, d)
        mods_lat = [m6[:nb, k].reshape(nb, 1, d) for k in range(6)]
        mods_ctx = [m6[nb:nb + 1, k].reshape(1, 1, d) for k in range(6)]
        w_in_l = bf(w_in[l])
        lru_p = (conv_w[l], conv_b[l], lru_wa[l], lru_ba[l], lru_wx[l], lru_bx[l], lru_lambda[l])
        zeros = jnp.zeros((nb, d_lru), F32)

        proj_c = _mixer_proj(cx, mods_ctx, norm1_g[l], w_in_l, ctx_len)
        z_c, htf, htb = _lru(proj_c, d_f, d_f + d_lru, *lru_p, zeros, zeros,
                             seq=ctx_len, conv_row=ctx_len, want_z=not last)
        w_fo_l, w_ro_l, w_o_l = bf(w_fourier_out[l]), bf(w_lru_out[l]), bf(w_out[l])
        if not last:
            cx = _mixer_out(cx, proj_c, z_c, mods_ctx, w_fo_l, w_ro_l, w_o_l,
                            ctx_len, d_f, d_lru, d)

        proj_l = _mixer_proj(lat, mods_lat, norm1_g[l], w_in_l, seq)
        z_l, _, _ = _lru(proj_l, d_f, d_f + d_lru, *lru_p, htf, htb,
                         seq=seq, conv_row=GRID_W, want_z=True)
        lat = _mixer_out(lat, proj_l, z_l, mods_lat, w_fo_l, w_ro_l, w_o_l, seq, d_f, d_lru, d)

        j = l // 2
        if l % 2 == 0:
            wg, wu, wd = _ffn_weights(ffn_w_gate[j:j + 1], ffn_w_up[j:j + 1],
                                      ffn_w_down[j:j + 1], FFN_TILE_F)
            if not last:
                cx = _dense_ffn(cx, mods_ctx, norm2_g[l], wg, wu, wd, ctx_len)
            lat = _dense_ffn(lat, mods_lat, norm2_g[l], wg, wu, wd, seq)
            if last:
                out = _final_norm(lat, final_norm_g)
        else:
            wg, wu, wd = _ffn_weights(moe_w_gate[j], moe_w_up[j], moe_w_down[j], MOE_TILE_F)
            if not last:
                cx = _moe_ffn(cx, mods_ctx, norm2_g[l], moe_router[j], wg, wu, wd,
                              final_norm_g, ctx_len, False)
            lat = _moe_ffn(lat, mods_lat, norm2_g[l], moe_router[j], wg, wu, wd,
                           final_norm_g, seq, last)
            out = lat
    return out.reshape(nb, seq, d)
```
